```python
import math
import jax
import jax.numpy as jnp
from jax import lax
import numpy as np

D_MODEL = 1024
BATCH = 8
SEQ = 2048
DEPTH = 1
DEC_BATCH = 8
DEC_SEQ = 16
PAST_LEN = 4096

CHUNK = 64
CONV_W = 4
LRU_WIDTH = 1024
LRU_HEADS = 8
LRU_BLOCK = LRU_WIDTH // LRU_HEADS
LRU_C = 8.0
D_INNER = 2 * D_MODEL
SSD_HEAD_DIM = 64
SSD_HEADS = D_INNER // SSD_HEAD_DIM
SSD_GROUPS = 4
HEADS_PER_GROUP = SSD_HEADS // SSD_GROUPS
D_STATE = 128
SSD_CONV_DIM = D_INNER + 2 * SSD_GROUPS * D_STATE
N_BRANCH = 2
FFN_DIM = 2816
EPS = 1e-6
IN_COLS = 2 * LRU_WIDTH + D_INNER + SSD_CONV_DIM + SSD_HEADS + N_BRANCH * D_MODEL

kernel_name = 'hybrid_rglru_ssd_macaron_stream_step'


def _rmsnorm(x, g):
    xf = x.astype(jnp.float32)
    inv = lax.rsqrt(jnp.mean(xf * xf, axis=-1, keepdims=True) + EPS)
    return (xf * inv).astype(x.dtype) * g


def _swiglu(x, w_up, w_down):
    gate, up = jnp.split(x @ w_up, 2, axis=-1)
    return (jax.nn.silu(gate) * up) @ w_down


def _causal_conv(x, buf, w, b):
    L = x.shape[1]
    xp = jnp.concatenate([buf, x], axis=1)
    y = b + xp[:, 0:L, :] * w[0]
    for k in range(1, CONV_W):
        y = y + xp[:, k:k + L, :] * w[k]
    return y, xp[:, L:, :]


def _rglru(x, h0, w_rg, b_rg, w_ig, b_ig, lam):
    nb, L, _ = x.shape
    xb = x.reshape(nb, L, LRU_HEADS, LRU_BLOCK)
    r = jax.nn.sigmoid(jnp.einsum('blhi,hij->blhj', xb, w_rg).reshape(nb, L, LRU_WIDTH) + b_rg)
    i = jax.nn.sigmoid(jnp.einsum('blhi,hij->blhj', xb, w_ig).reshape(nb, L, LRU_WIDTH) + b_ig)
    log_a = -LRU_C * r.astype(jnp.float32) * jax.nn.softplus(-lam.astype(jnp.float32))
    a = jnp.exp(log_a)
    u = jnp.sqrt(-jnp.expm1(2.0 * log_a)) * (i * x).astype(jnp.float32)
    u = u.at[:, 0].add(a[:, 0] * h0.astype(jnp.float32))

    def combine(left, right):
        a1, b1 = left
        a2, b2 = right
        return a1 * a2, a2 * b1 + b2

    _, h = lax.associative_scan(combine, (a, u), axis=1)
    return h.astype(x.dtype), h[:, -1].astype(x.dtype)


def _segsum(x):
    T = x.shape[-1]
    xr = jnp.broadcast_to(x[..., :, None], x.shape + (T,))
    strict = jnp.tril(jnp.ones((T, T), dtype=bool), -1)
    cs = jnp.cumsum(jnp.where(strict, xr, 0.0), axis=-2)
    return jnp.where(jnp.tril(jnp.ones((T, T), dtype=bool)), cs, -jnp.inf)


def _ssd_scan(x, dt, A, Bm, Cm, h0):
    nb, L = x.shape[0], x.shape[1]
    cs = min(CHUNK, L)
    nc = L // cs
    G, E, P, N = SSD_GROUPS, HEADS_PER_GROUP, SSD_HEAD_DIM, D_STATE
    x = x.reshape(nb, nc, cs, G, E, P)
    dt = dt.reshape(nb, nc, cs, G, E)
    Bm = Bm.reshape(nb, nc, cs, G, N)
    Cm = Cm.reshape(nb, nc, cs, G, N)
    dA = jnp.moveaxis(dt * A.reshape(G, E), 2, -1)
    xd = x * dt[..., None]
    A_cs = jnp.cumsum(dA, axis=-1)
    Lmat = jnp.exp(_segsum(dA))
    CB = jnp.einsum('bclgn,bcsgn->bcgls', Cm, Bm)
    y_diag = jnp.einsum('bcgls,bcgels,bcsgep->bclgep', CB, Lmat, xd)
    decay_states = jnp.exp(A_cs[..., -1:] - A_cs)
    states = jnp.einsum('bcsgn,bcges,bcsgep->bcgepn', Bm, decay_states, xd)
    states = jnp.concatenate([h0.reshape(nb, G, E, P, N)[:, None], states], axis=1)
    chunk_decay = jnp.pad(A_cs[..., -1], ((0, 0), (1, 0), (0, 0), (0, 0)))
    decay_chunk = jnp.exp(_segsum(jnp.moveaxis(chunk_decay, 1, -1)))
    new_states = jnp.einsum('bgezc,bcgepn->bzgepn', decay_chunk, states)
    states_in, final = new_states[:, :-1], new_states[:, -1]
    y_off = jnp.einsum('bclgn,bcgepn,bcgel->bclgep', Cm, states_in, jnp.exp(A_cs))
    y = (y_diag + y_off).reshape(nb, L, SSD_HEADS, P)
    return y, final.reshape(nb, SSD_HEADS, P, N)


def _mixer(u, lru_conv_buf, lru_h0, ssd_conv_buf, ssd_h0, p):
    nb, L, _ = u.shape
    proj = u @ p['w_in']
    sizes = [LRU_WIDTH, LRU_WIDTH, D_INNER, SSD_CONV_DIM, SSD_HEADS, N_BRANCH * D_MODEL]
    offs = np.cumsum(sizes)[:-1].tolist()
    lru_x, lru_gate, z, xbc, dt_raw, gate_logits = jnp.split(proj, offs, axis=-1)

    xc, lru_conv_new = _causal_conv(lru_x, lru_conv_buf, p['lru_conv_w'], p['lru_conv_b'])
    h, lru_h_new = _rglru(xc, lru_h0, p['lru_w_rg'], p['lru_b_rg'], p['lru_w_ig'], p['lru_b_ig'], p['lru_lambda'])
    y_lru = jax.nn.gelu(lru_gate) * h

    xbc_c, ssd_conv_new = _causal_conv(xbc, ssd_conv_buf, p['ssd_conv_w'], p['ssd_conv_b'])
    xbc_c = jax.nn.silu(xbc_c)
    xs, Bm, Cm = jnp.split(xbc_c, [D_INNER, D_INNER + SSD_GROUPS * D_STATE], axis=-1)
    xs_h = xs.reshape(nb, L, SSD_HEADS, SSD_HEAD_DIM).astype(jnp.float32)
    dt = jax.nn.softplus(dt_raw.astype(jnp.float32) + p['ssd_dt_bias'].astype(jnp.float32))
    A = -jnp.exp(p['ssd_A_log'].astype(jnp.float32))
    y, ssd_h_new = _ssd_scan(
        xs_h, dt, A,
        Bm.reshape(nb, L, SSD_GROUPS, D_STATE).astype(jnp.float32),
        Cm.reshape(nb, L, SSD_GROUPS, D_STATE).astype(jnp.float32),
        ssd_h0.astype(jnp.float32))
    y = y + p['ssd_D'].astype(jnp.float32)[:, None] * xs_h
    y = y.reshape(nb, L, D_INNER).astype(u.dtype) * jax.nn.silu(z)
    y = _rmsnorm(y.reshape(nb, L, SSD_GROUPS, D_INNER // SSD_GROUPS),
                 p['ssd_norm_g'].reshape(SSD_GROUPS, D_INNER // SSD_GROUPS)).reshape(nb, L, D_INNER)

    g_lru, g_ssd = jnp.split(jax.nn.sigmoid(gate_logits + p['b_gate']), 2, axis=-1)
    m = g_lru * (y_lru @ p['w_br_lru']) + g_ssd * (y @ p['w_br_ssd'])
    out = m @ p['w_out']
    return out, lru_conv_new, lru_h_new, ssd_conv_new, ssd_h_new.astype(u.dtype)


def _layer(x, lru_conv_buf, lru_h0, ssd_conv_buf, ssd_h0, p):
    x = x + 0.5 * _swiglu(_rmsnorm(x, p['norm_ffn1_g']), p['ffn1_w_up'], p['ffn1_w_down'])
    mix, lcv, lh, scv, sh = _mixer(_rmsnorm(x, p['norm_mix_g']), lru_conv_buf, lru_h0, ssd_conv_buf, ssd_h0, p)
    x = x + mix
    x = x + 0.5 * _swiglu(_rmsnorm(x, p['norm_ffn2_g']), p['ffn2_w_up'], p['ffn2_w_down'])
    return x, (lcv, lh, scv, sh)


def setup_inputs(seed: int = 0) -> dict:
    key = jax.random.key(seed)
    ks = jax.random.split(key, 40)
    f32 = jnp.float32

    def nrm(k, shape, scale):
        return jax.random.normal(k, shape, f32) * scale

    u_lam = jax.random.uniform(ks[10], (DEPTH, LRU_WIDTH), f32, minval=0.9, maxval=0.999)
    dt0 = jnp.exp(jax.random.uniform(ks[14], (DEPTH, SSD_HEADS), f32,
                                     minval=math.log(1e-3), maxval=math.log(1e-1)))
    return {
        'x_prompt': nrm(ks[0], (BATCH, SEQ, D_MODEL), 1.0),
        'x_sample': nrm(ks[1], (DEC_BATCH, DEC_SEQ, D_MODEL), 1.0),
        'state_lru_conv': nrm(ks[2], (DEPTH, DEC_BATCH, CONV_W - 1, LRU_WIDTH), 1.0),
        'state_lru_h': nrm(ks[3], (DEPTH, DEC_BATCH, LRU_WIDTH), 0.5),
        'state_ssd_conv': nrm(ks[4], (DEPTH, DEC_BATCH, CONV_W - 1, SSD_CONV_DIM), 1.0),
        'state_ssd_h': nrm(ks[5], (DEPTH, DEC_BATCH, SSD_HEADS, SSD_HEAD_DIM, D_STATE), 0.1),
        'norm_ffn1_g': 1.0 + nrm(ks[6], (DEPTH, D_MODEL), 0.02),
        'ffn1_w_up': nrm(ks[7], (DEPTH, D_MODEL, 2 * FFN_DIM), D_MODEL ** -0.5),
        'ffn1_w_down': nrm(ks[8], (DEPTH, FFN_DIM, D_MODEL), FFN_DIM ** -0.5),
        'norm_mix_g': 1.0 + nrm(ks[9], (DEPTH, D_MODEL), 0.02),
        'w_in': nrm(ks[11], (DEPTH, D_MODEL, IN_COLS), D_MODEL ** -0.5),
        'b_gate': nrm(ks[12], (DEPTH, N_BRANCH * D_MODEL), 0.01),
        'lru_conv_w': nrm(ks[13], (DEPTH, CONV_W, LRU_WIDTH), CONV_W ** -0.5),
        'lru_conv_b': nrm(ks[15], (DEPTH, LRU_WIDTH), 0.01),
        'lru_w_rg': nrm(ks[16], (DEPTH, LRU_HEADS, LRU_BLOCK, LRU_BLOCK), LRU_BLOCK ** -0.5),
        'lru_b_rg': nrm(ks[17], (DEPTH, LRU_WIDTH), 0.01),
        'lru_w_ig': nrm(ks[18], (DEPTH, LRU_HEADS, LRU_BLOCK, LRU_BLOCK), LRU_BLOCK ** -0.5),
        'lru_b_ig': nrm(ks[19], (DEPTH, LRU_WIDTH), 0.01),
        'lru_lambda': jnp.log(u_lam) - jnp.log1p(-u_lam),
        'ssd_conv_w': nrm(ks[20], (DEPTH, CONV_W, SSD_CONV_DIM), CONV_W ** -0.5),
        'ssd_conv_b': nrm(ks[21], (DEPTH, SSD_CONV_DIM), 0.01),
        'ssd_dt_bias': dt0 + jnp.log(-jnp.expm1(-dt0)),
        'ssd_A_log': jnp.log(jax.random.uniform(ks[22], (DEPTH, SSD_HEADS), f32, minval=1.0, maxval=16.0)),
        'ssd_D': 1.0 + nrm(ks[23], (DEPTH, SSD_HEADS), 0.02),
        'ssd_norm_g': 1.0 + nrm(ks[24], (DEPTH, D_INNER), 0.02),
        'w_br_lru': nrm(ks[25], (DEPTH, LRU_WIDTH, D_MODEL), LRU_WIDTH ** -0.5),
        'w_br_ssd': nrm(ks[26], (DEPTH, D_INNER, D_MODEL), D_INNER ** -0.5),
        'w_out': nrm(ks[27], (DEPTH, D_MODEL, D_MODEL), D_MODEL ** -0.5),
        'norm_ffn2_g': 1.0 + nrm(ks[28], (DEPTH, D_MODEL), 0.02),
        'ffn2_w_up': nrm(ks[29], (DEPTH, D_MODEL, 2 * FFN_DIM), D_MODEL ** -0.5),
        'ffn2_w_down': nrm(ks[30], (DEPTH, FFN_DIM, D_MODEL), FFN_DIM ** -0.5),
        'norm_final_g': 1.0 + nrm(ks[31], (D_MODEL,), 0.02),
    }


def reference(x_prompt, x_sample, state_lru_conv, state_lru_h, state_ssd_conv, state_ssd_h,
              norm_ffn1_g, ffn1_w_up, ffn1_w_down, norm_mix_g, w_in, b_gate,
              lru_conv_w, lru_conv_b, lru_w_rg, lru_b_rg, lru_w_ig, lru_b_ig, lru_lambda,
              ssd_conv_w, ssd_conv_b, ssd_dt_bias, ssd_A_log, ssd_D, ssd_norm_g,
              w_br_lru, w_br_ssd, w_out, norm_ffn2_g, ffn2_w_up, ffn2_w_down, norm_final_g):
    nbp = x_prompt.shape[0]
    dtype = x_prompt.dtype
    xp, xs = x_prompt, x_sample
    new_p, new_s = [], []
    for l in range(DEPTH):
        p = {
            'norm_ffn1_g': norm_ffn1_g[l], 'ffn1_w_up': ffn1_w_up[l], 'ffn1_w_down': ffn1_w_down[l],
            'norm_mix_g': norm_mix_g[l], 'w_in': w_in[l], 'b_gate': b_gate[l],
            'lru_conv_w': lru_conv_w[l], 'lru_conv_b': lru_conv_b[l],
            'lru_w_rg': lru_w_rg[l], 'lru_b_rg': lru_b_rg[l],
            'lru_w_ig': lru_w_ig[l], 'lru_b_ig': lru_b_ig[l], 'lru_lambda': lru_lambda[l],
            'ssd_conv_w': ssd_conv_w[l], 'ssd_conv_b': ssd_conv_b[l],
            'ssd_dt_bias': ssd_dt_bias[l], 'ssd_A_log': ssd_A_log[l], 'ssd_D': ssd_D[l],
            'ssd_norm_g': ssd_norm_g[l], 'w_br_lru': w_br_lru[l], 'w_br_ssd': w_br_ssd[l],
            'w_out': w_out[l], 'norm_ffn2_g': norm_ffn2_g[l],
            'ffn2_w_up': ffn2_w_up[l], 'ffn2_w_down': ffn2_w_down[l],
        }
        xp, st_p = _layer(
            xp,
            jnp.zeros((nbp, CONV_W - 1, LRU_WIDTH), dtype),
            jnp.zeros((nbp, LRU_WIDTH), dtype),
            jnp.zeros((nbp, CONV_W - 1, SSD_CONV_DIM), dtype),
            jnp.zeros((nbp, SSD_HEADS, SSD_HEAD_DIM, D_STATE), dtype),
            p)
        xs, st_s = _layer(xs, state_lru_conv[l], state_lru_h[l], state_ssd_conv[l], state_ssd_h[l], p)
        new_p.append(st_p)
        new_s.append(st_s)
    y_prompt = _rmsnorm(xp, norm_final_g)
    y_sample = _rmsnorm(xs, norm_final_g)
    lru_conv_p = jnp.stack([s[0] for s in new_p])
    lru_h_p = jnp.stack([s[1] for s in new_p])
    ssd_conv_p = jnp.stack([s[2] for s in new_p])
    ssd_h_p = jnp.stack([s[3] for s in new_p])
    lru_conv_s = jnp.stack([s[0] for s in new_s])
    lru_h_s = jnp.stack([s[1] for s in new_s])
    ssd_conv_s = jnp.stack([s[2] for s in new_s])
    ssd_h_s = jnp.stack([s[3] for s in new_s])
    return (y_prompt, y_sample, lru_conv_p, lru_h_p, ssd_conv_p, ssd_h_p,
            lru_conv_s, lru_h_s, ssd_conv_s, ssd_h_s)
```

```python
import functools

import jax
import jax.numpy as jnp
from jax import lax
from jax.experimental import pallas as pl
from jax.experimental.pallas import tpu as pltpu

D_MODEL = 1024
FFN_DIM = 2816
LRU_WIDTH = 1024
LRU_HEADS = 8
LRU_BLOCK = LRU_WIDTH // LRU_HEADS
LRU_C = 8.0
D_INNER = 2048
SSD_HEADS = 32
SSD_HEAD_DIM = 64
SSD_GROUPS = 4
GROUP_WIDTH = D_INNER // SSD_GROUPS
D_STATE = 128
SSD_CONV_DIM = D_INNER + 2 * SSD_GROUPS * D_STATE
CONV_W = 4
EPS = 1e-6

LANES = 128
SUBLANES = 8
TAIL_ROWS = SUBLANES
VMEM_LIMIT_BYTES = 56 * 1024 * 1024

FFN_CHUNK = 256
SSD_CHUNK = 128
BF16 = jnp.bfloat16
F32 = jnp.float32


def _dot(a, b):
    return jnp.dot(a, b, preferred_element_type=F32)


def _dot_nt(a, b):
    return lax.dot_general(a, b, (((1,), (1,)), ((), ())), preferred_element_type=F32)


def _dot_tn(a, b):
    return lax.dot_general(a, b, (((0,), (0,)), ((), ())), preferred_element_type=F32)


def _split_bf16(x):
    hi = x.astype(BF16)
    lo = (x - hi.astype(F32)).astype(BF16)
    return hi, lo


def _rms(x):
    return x * lax.rsqrt(jnp.mean(x * x, axis=-1, keepdims=True) + EPS)


def _sigmoid(x):
    return 1.0 / (1.0 + jnp.exp(-x))


def _softplus(x):
    return jnp.maximum(x, 0.0) + jnp.log1p(jnp.exp(-jnp.abs(x)))


def _gelu_tanh(x):
    return 0.5 * x * (1.0 + jnp.tanh(0.7978845608028654 * (x + 0.044715 * (x * x * x))))


def _resident(shape):
    return pl.BlockSpec(shape, lambda *_: (0,) * len(shape), pipeline_mode=pl.Buffered(1))


def _ffn_body(x_ref, g_ref, wup_ref, wdn_ref, gf_ref, o_ref, act_ref, *, final_norm):
    x = x_ref[...]
    xb = (_rms(x) * g_ref[...]).astype(BF16)
    for c in range(FFN_DIM // FFN_CHUNK):
        lo = c * FFN_CHUNK
        gate = _dot(xb, wup_ref[:, lo:lo + FFN_CHUNK])
        up = _dot(xb, wup_ref[:, FFN_DIM + lo:FFN_DIM + lo + FFN_CHUNK])
        act_ref[:, lo:lo + FFN_CHUNK] = (gate * _sigmoid(gate) * up).astype(BF16)
    y = x + 0.5 * _dot(act_ref[...], wdn_ref[...])
    if final_norm:
        y = _rms(y) * gf_ref[...]
    o_ref[...] = y


def _ffn(x2d, g, w_up, w_down, g_final, *, final_norm, tile_rows):
    rows = x2d.shape[0]
    assert rows % tile_rows == 0
    row_spec = pl.BlockSpec((tile_rows, D_MODEL), lambda i: (i, 0))
    return pl.pallas_call(
        functools.partial(_ffn_body, final_norm=final_norm),
        grid=(rows // tile_rows,),
        in_specs=[row_spec, _resident((1, D_MODEL)), _resident((D_MODEL, 2 * FFN_DIM)),
                  _resident((FFN_DIM, D_MODEL)), _resident((1, D_MODEL))],
        out_specs=row_spec,
        out_shape=jax.ShapeDtypeStruct((rows, D_MODEL), F32),
        scratch_shapes=[pltpu.VMEM((tile_rows, FFN_DIM), BF16)],
        compiler_params=pltpu.CompilerParams(dimension_semantics=("arbitrary",),
                                             vmem_limit_bytes=VMEM_LIMIT_BYTES),
        name="ffn_final" if final_norm else "ffn",
    )(x2d, g, w_up, w_down, g_final)


def _causal_conv(x, tail_ref, seq, buf_ref, w_ref, b_ref):
    rows = x.shape[0]
    buf_ref[0:TAIL_ROWS, :] = tail_ref[seq]
    buf_ref[TAIL_ROWS:TAIL_ROWS + rows, :] = x
    y = b_ref[...] + x * w_ref[CONV_W - 1:CONV_W, :]
    for k in range(CONV_W - 1):
        start = TAIL_ROWS - (CONV_W - 1) + k
        y = y + buf_ref[start:start + rows, :] * w_ref[k:k + 1, :]
    tail_ref[seq] = buf_ref[rows:rows + TAIL_ROWS, :]
    return y


def _linear_scan(a, b, h0):
    rows, width = a.shape
    tiles = rows // SUBLANES
    a3 = a.reshape(tiles, SUBLANES, width)
    b3 = b.reshape(tiles, SUBLANES, width)
    sub = lax.broadcasted_iota(jnp.int32, (tiles, SUBLANES, width), 1)
    shift = 1
    while shift < SUBLANES:
        keep = sub >= shift
        a_prev = pltpu.roll(a3, shift, axis=1)
        b_prev = pltpu.roll(b3, shift, axis=1)
        b3 = jnp.where(keep, a3 * b_prev + b3, b3)
        a3 = jnp.where(keep, a3 * a_prev, a3)
        shift *= 2
    out = []
    carry = h0
    for t in range(tiles):
        h = a3[t] * carry + b3[t]
        out.append(h)
        carry = h[SUBLANES - 1:SUBLANES, :]
    return jnp.concatenate(out, axis=0), carry


def _ssd_seq(xs, bm, cm, dt, dt_t, a_row, a_col, state_ref, seq, expand):
    rows = xs.shape[0]
    q = min(SSD_CHUNK, rows)
    da = dt * a_row
    da_t = dt_t * a_col
    li = lax.broadcasted_iota(jnp.int32, (q, q), 0)
    si = lax.broadcasted_iota(jnp.int32, (q, q), 1)
    causal = li >= si
    lower = causal.astype(BF16)
    upper = (li <= si).astype(BF16)
    lane = lax.broadcasted_iota(jnp.int32, (q, LANES), 1)
    first_head = lane < SSD_HEAD_DIM
    ys = []
    for c in range(rows // q):
        r0 = c * q
        da_c = da[r0:r0 + q, :]
        dt_c = dt[r0:r0 + q, :]
        da_hi, da_lo = _split_bf16(da_c)
        cs = _dot(lower, da_hi) + _dot(lower, da_lo)
        dat_hi, dat_lo = _split_bf16(da_t[:, r0:r0 + q])
        cs_t = _dot(dat_hi, upper) + _dot(dat_lo, upper)
        dt_rows = dt_t[:, r0:r0 + q]
        total = cs[q - 1:q, :]
        from_start = jnp.exp(cs)
        to_end = jnp.exp(total - cs) * dt_c
        fs_hi, fs_lo = _split_bf16(from_start)
        from_start_x = _dot(fs_hi, expand) + _dot(fs_lo, expand)
        te_hi, te_lo = _split_bf16(to_end)
        to_end_x = _dot(te_hi, expand) + _dot(te_lo, expand)
        xs_c = xs[r0:r0 + q, :]
        y_groups = []
        for g in range(SSD_GROUPS):
            n0 = g * D_STATE
            c0 = g * GROUP_WIDTH
            b_g = bm[r0:r0 + q, n0:n0 + D_STATE].astype(BF16)
            c_g = cm[r0:r0 + q, n0:n0 + D_STATE].astype(BF16)
            cb = _dot_nt(c_g, b_g)
            y_pairs = []
            for pair in range(GROUP_WIDTH // LANES):
                l0 = c0 + pair * LANES
                x_pair = xs_c[:, l0:l0 + LANES]
                acc = None
                for half in range(2):
                    h = l0 // SSD_HEAD_DIM + half
                    seg = cs[:, h:h + 1] - cs_t[h:h + 1, :]
                    decay = jnp.exp(jnp.where(causal, seg, -jnp.inf))
                    m = (cb * decay * dt_rows[h:h + 1, :]).astype(BF16)
                    keep = first_head if half == 0 else jnp.logical_not(first_head)
                    x_half = jnp.where(keep, x_pair, 0.0).astype(BF16)
                    part = _dot(m, x_half)
                    acc = part if acc is None else acc + part
                y_pairs.append(acc)
            y_diag = jnp.concatenate(y_pairs, axis=1)
            st = state_ref[seq, g]
            fs_g = from_start_x[:, c0:c0 + GROUP_WIDTH]
            y_off = _dot(c_g, st.astype(BF16)) * fs_g
            xw = (xs_c[:, c0:c0 + GROUP_WIDTH] * to_end_x[:, c0:c0 + GROUP_WIDTH]).astype(BF16)
            state_ref[seq, g] = st * fs_g[q - 1:q, :] + _dot_tn(b_g, xw)
            y_groups.append(y_diag + y_off)
        ys.append(jnp.concatenate(y_groups, axis=1))
    return ys[0] if len(ys) == 1 else jnp.concatenate(ys, axis=0)


def _mixer_body(*refs, nb, rows, has_state):
    if has_state:
        (x_ref, lru_tail_in, lru_h_in, ssd_tail_in, ssd_h_in, *refs) = refs
    else:
        (x_ref, *refs) = refs
    (g_ref, w_main, w_dt, w_dt_t, w_gate, b_gate, lconv_w, lconv_b, w_rgig, b_rg, b_ig, lam,
     sconv_w, sconv_b, dt_bias, dt_bias_t, a_log, a_log_t, d_skip, norm_g, w_br_lru, w_br_ssd, w_out,
     o_ref, lru_tail_out, lru_h_out, ssd_tail_out, ssd_h_out,
     lru_tail, lru_h, ssd_tail, ssd_state, lru_buf, ssd_buf) = refs

    step = pl.program_id(1)
    last = pl.num_programs(1) - 1

    @pl.when(step == 0)
    def _():
        if has_state:
            lru_tail[...] = lru_tail_in[...]
            lru_h[...] = lru_h_in[...]
            ssd_tail[...] = ssd_tail_in[...]
            for s in range(nb):
                for g in range(SSD_GROUPS):
                    ssd_state[s, g] = ssd_h_in[s, g * GROUP_WIDTH:(g + 1) * GROUP_WIDTH, :].T
        else:
            lru_tail[...] = jnp.zeros_like(lru_tail)
            lru_h[...] = jnp.zeros_like(lru_h)
            ssd_tail[...] = jnp.zeros_like(ssd_tail)
            ssd_state[...] = jnp.zeros_like(ssd_state)

    x = x_ref[...].reshape(nb * rows, D_MODEL)
    u = (_rms(x) * g_ref[...]).astype(BF16)

    lru_x = _dot(u, w_main[:, 0:LRU_WIDTH])
    xc = jnp.concatenate(
        [_causal_conv(lru_x[s * rows:(s + 1) * rows], lru_tail, s, lru_buf, lconv_w, lconv_b)
         for s in range(nb)], axis=0) if nb > 1 else _causal_conv(lru_x, lru_tail, 0, lru_buf, lconv_w, lconv_b)
    xc_b = xc.astype(BF16)
    gates = [_dot(xc_b[:, h * LRU_BLOCK:(h + 1) * LRU_BLOCK], w_rgig[h]) for h in range(LRU_HEADS)]
    r = _sigmoid(jnp.concatenate([gh[:, :LRU_BLOCK] for gh in gates], axis=1) + b_rg[...])
    i_gate = _sigmoid(jnp.concatenate([gh[:, LRU_BLOCK:] for gh in gates], axis=1) + b_ig[...])
    log_a = (-LRU_C) * r * _softplus(-lam[...])
    a = jnp.exp(log_a)
    b = jnp.sqrt(-jnp.tanh(log_a) * (a * a + 1.0)) * (i_gate * xc)
    hs = []
    for s in range(nb):
        h_s, h_last = _linear_scan(a[s * rows:(s + 1) * rows], b[s * rows:(s + 1) * rows], lru_h[s])
        lru_h[s] = h_last
        hs.append(h_s)
    h_all = hs[0] if nb == 1 else jnp.concatenate(hs, axis=0)
    lru_gate = _dot(u, w_main[:, LRU_WIDTH:2 * LRU_WIDTH])
    y_lru = (_gelu_tanh(lru_gate) * h_all).astype(BF16)
    br_lru = _dot(y_lru, w_br_lru[...])

    off_z = 2 * LRU_WIDTH
    off_xbc = off_z + D_INNER
    xbc = _dot(u, w_main[:, off_xbc:off_xbc + SSD_CONV_DIM])
    xbc_c = jnp.concatenate(
        [_causal_conv(xbc[s * rows:(s + 1) * rows], ssd_tail, s, ssd_buf, sconv_w, sconv_b)
         for s in range(nb)], axis=0) if nb > 1 else _causal_conv(xbc, ssd_tail, 0, ssd_buf, sconv_w, sconv_b)
    xbc_c = xbc_c * _sigmoid(xbc_c)
    xs = xbc_c[:, 0:D_INNER]
    bm = xbc_c[:, D_INNER:D_INNER + SSD_GROUPS * D_STATE]
    cm = xbc_c[:, D_INNER + SSD_GROUPS * D_STATE:]
    dt = _softplus(_dot(u, w_dt[...]) + dt_bias[...])
    a_row = -jnp.exp(a_log[...])
    a_col = -jnp.exp(a_log_t[...])
    head_of_col = lax.broadcasted_iota(jnp.int32, (SSD_HEADS, D_INNER), 1) // SSD_HEAD_DIM
    expand = (lax.broadcasted_iota(jnp.int32, (SSD_HEADS, D_INNER), 0) == head_of_col).astype(BF16)
    ys = []
    for s in range(nb):
        u_s = u[s * rows:(s + 1) * rows]
        dt_t = _softplus(_dot_nt(w_dt_t[...], u_s) + dt_bias_t[...])
        ys.append(_ssd_seq(xs[s * rows:(s + 1) * rows], bm[s * rows:(s + 1) * rows],
                           cm[s * rows:(s + 1) * rows], dt[s * rows:(s + 1) * rows], dt_t,
                           a_row, a_col, ssd_state, s, expand))
    y = ys[0] if nb == 1 else jnp.concatenate(ys, axis=0)
    y = y + d_skip[...] * xs
    z = _dot(u, w_main[:, off_z:off_z + D_INNER])
    y = y * (z * _sigmoid(z))
    y = jnp.concatenate(
        [_rms(y[:, g * GROUP_WIDTH:(g + 1) * GROUP_WIDTH]) for g in range(SSD_GROUPS)], axis=1) * norm_g[...]
    br_ssd = _dot(y.astype(BF16), w_br_ssd[...])

    gl = _sigmoid(_dot(u, w_gate[...]) + b_gate[...])
    m = gl[:, 0:D_MODEL] * br_lru + gl[:, D_MODEL:] * br_ssd
    out = x + _dot(m.astype(BF16), w_out[...])
    o_ref[...] = out.reshape(nb, rows, D_MODEL)

    @pl.when(step == last)
    def _():
        lru_tail_out[...] = lru_tail[...]
        lru_h_out[...] = lru_h[...]
        ssd_tail_out[...] = ssd_tail[...]
        for s in range(nb):
            for g in range(SSD_GROUPS):
                ssd_h_out[s, g * GROUP_WIDTH:(g + 1) * GROUP_WIDTH, :] = ssd_state[s, g].T


def _mixer(x, states, weights, *, nb, rows):
    batch, length, _ = x.shape
    assert batch % nb == 0 and length % rows == 0 and rows % SUBLANES == 0
    has_state = states is not None
    grid = (batch // nb, length // rows)

    def per_seq(shape):
        return pl.BlockSpec((nb,) + shape, lambda i, j: (i,) + (0,) * len(shape), pipeline_mode=pl.Buffered(1))

    x_spec = pl.BlockSpec((nb, rows, D_MODEL), lambda i, j: (i, j, 0))
    state_specs = [per_seq((TAIL_ROWS, LRU_WIDTH)), per_seq((1, LRU_WIDTH)),
                   per_seq((TAIL_ROWS, SSD_CONV_DIM)), per_seq((D_INNER, D_STATE))]
    in_specs = [x_spec] + (state_specs if has_state else []) + [_resident(w.shape) for w in weights]
    out_shape = [jax.ShapeDtypeStruct((batch, length, D_MODEL), F32),
                 jax.ShapeDtypeStruct((batch, TAIL_ROWS, LRU_WIDTH), F32),
                 jax.ShapeDtypeStruct((batch, 1, LRU_WIDTH), F32),
                 jax.ShapeDtypeStruct((batch, TAIL_ROWS, SSD_CONV_DIM), F32),
                 jax.ShapeDtypeStruct((batch, D_INNER, D_STATE), F32)]
    scratch = [pltpu.VMEM((nb, TAIL_ROWS, LRU_WIDTH), F32),
               pltpu.VMEM((nb, 1, LRU_WIDTH), F32),
               pltpu.VMEM((nb, TAIL_ROWS, SSD_CONV_DIM), F32),
               pltpu.VMEM((nb, SSD_GROUPS, D_STATE, GROUP_WIDTH), F32),
               pltpu.VMEM((TAIL_ROWS + rows, LRU_WIDTH), F32),
               pltpu.VMEM((TAIL_ROWS + rows, SSD_CONV_DIM), F32)]
    args = (x,) + (tuple(states) if has_state else ()) + tuple(weights)
    return pl.pallas_call(
        functools.partial(_mixer_body, nb=nb, rows=rows, has_state=has_state),
        grid=grid,
        in_specs=in_specs,
        out_specs=[x_spec] + state_specs,
        out_shape=out_shape,
        scratch_shapes=scratch,
        compiler_params=pltpu.CompilerParams(dimension_semantics=("arbitrary", "arbitrary"),
                                             vmem_limit_bytes=VMEM_LIMIT_BYTES),
        name="mixer_state" if has_state else "mixer",
    )(*args)


def _pad_tail(conv_state):
    return jnp.pad(conv_state, ((0, 0), (TAIL_ROWS - (CONV_W - 1), 0), (0, 0)))


def kernel(x_prompt, x_sample, state_lru_conv, state_lru_h, state_ssd_conv, state_ssd_h, norm_ffn1_g, ffn1_w_up, ffn1_w_down, norm_mix_g, w_in, b_gate, lru_conv_w, lru_conv_b, lru_w_rg, lru_b_rg, lru_w_ig, lru_b_ig, lru_lambda, ssd_conv_w, ssd_conv_b, ssd_dt_bias, ssd_A_log, ssd_D, ssd_norm_g, w_br_lru, w_br_ssd, w_out, norm_ffn2_g, ffn2_w_up, ffn2_w_down, norm_final_g):
    depth = w_in.shape[0]
    assert depth == 1
    l = 0
    row = lambda v: v.reshape(1, -1)
    off_dt = 2 * LRU_WIDTH + D_INNER + SSD_CONV_DIM
    w_in_l = w_in[l]
    w_dt = w_in_l[:, off_dt:off_dt + SSD_HEADS]
    mixer_weights = (
        row(norm_mix_g[l]),
        w_in_l[:, :off_dt].astype(BF16),
        w_dt.astype(BF16),
        w_dt.T.astype(BF16),
        w_in_l[:, off_dt + SSD_HEADS:].astype(BF16),
        row(b_gate[l]),
        lru_conv_w[l], row(lru_conv_b[l]),
        jnp.concatenate([lru_w_rg[l], lru_w_ig[l]], axis=-1).astype(BF16),
        row(lru_b_rg[l]), row(lru_b_ig[l]), row(lru_lambda[l]),
        ssd_conv_w[l], row(ssd_conv_b[l]),
        row(ssd_dt_bias[l]), ssd_dt_bias[l].reshape(-1, 1),
        row(ssd_A_log[l]), ssd_A_log[l].reshape(-1, 1),
        row(jnp.repeat(ssd_D[l], SSD_HEAD_DIM)),
        row(ssd_norm_g[l]),
        w_br_lru[l].astype(BF16), w_br_ssd[l].astype(BF16), w_out[l].astype(BF16),
    )
    ffn1 = (row(norm_ffn1_g[l]), ffn1_w_up[l].astype(BF16), ffn1_w_down[l].astype(BF16))
    ffn2 = (row(norm_ffn2_g[l]), ffn2_w_up[l].astype(BF16), ffn2_w_down[l].astype(BF16))
    g_final = row(norm_final_g)

    def layer(x, states, *, ffn_rows, nb, rows):
        batch, length, _ = x.shape
        h = _ffn(x.reshape(batch * length, D_MODEL), *ffn1, g_final, final_norm=False, tile_rows=ffn_rows)
        h, lru_tail, lru_h, ssd_tail, ssd_h = _mixer(h.reshape(batch, length, D_MODEL), states, mixer_weights,
                                                     nb=nb, rows=rows)
        y = _ffn(h.reshape(batch * length, D_MODEL), *ffn2, g_final, final_norm=True, tile_rows=ffn_rows)
        new_states = (lru_tail[None, :, TAIL_ROWS - (CONV_W - 1):, :],
                      lru_h.reshape(1, batch, LRU_WIDTH),
                      ssd_tail[None, :, TAIL_ROWS - (CONV_W - 1):, :],
                      ssd_h.reshape(1, batch, SSD_HEADS, SSD_HEAD_DIM, D_STATE))
        return y.reshape(batch, length, D_MODEL), new_states

    y_prompt, st_p = layer(x_prompt, None, ffn_rows=512, nb=1, rows=256)
    dec_batch, dec_len, _ = x_sample.shape
    sample_states = (_pad_tail(state_lru_conv[l]), state_lru_h[l].reshape(dec_batch, 1, LRU_WIDTH),
                     _pad_tail(state_ssd_conv[l]), state_ssd_h[l].reshape(dec_batch, D_INNER, D_STATE))
    y_sample, st_s = layer(x_sample, sample_states, ffn_rows=dec_batch * dec_len, nb=dec_batch // 2, rows=dec_len)
    return (y_prompt, y_sample) + st_p + st_s
```

```python
import functools
import math

import jax
import jax.numpy as jnp
from jax import lax
from jax.experimental import pallas as pl
from jax.experimental.pallas import tpu as pltpu

D_MODEL = 1024
FFN_DIM = 2816
LRU_WIDTH = 1024
LRU_HEADS = 8
LRU_BLOCK = LRU_WIDTH // LRU_HEADS
LRU_C = 8.0
D_INNER = 2048
SSD_HEADS = 32
SSD_HEAD_DIM = 64
SSD_GROUPS = 4
GROUP_WIDTH = D_INNER // SSD_GROUPS
D_STATE = 128
SSD_CONV_DIM = D_INNER + 2 * SSD_GROUPS * D_STATE
CONV_W = 4
EPS = 1e-6
LOG2E = 1.0 / math.log(2.0)

LANES = 128
SUBLANES = 8
MXU_COLS = 256
SLABS_PER_TILE = MXU_COLS // LANES
TAIL_ROWS = SUBLANES
VMEM_LIMIT_BYTES = 56 * 1024 * 1024

SSD_CHUNK = 128
BF16 = jnp.bfloat16
F32 = jnp.float32

LRU_X_TILE = 0
LRU_TILES = LRU_WIDTH // MXU_COLS
LRU_GATE_TILE = LRU_X_TILE + LRU_TILES
Z_TILE = LRU_GATE_TILE + LRU_TILES
Z_TILES = D_INNER // MXU_COLS
XBC_TILE = Z_TILE + Z_TILES
XBC_TILES = SSD_CONV_DIM // MXU_COLS
OUT_TILES = D_MODEL // MXU_COLS


def _dot(a, b):
    return jnp.dot(a, b, preferred_element_type=F32)


def _dot_nt(a, b):
    return lax.dot_general(a, b, (((1,), (1,)), ((), ())), preferred_element_type=F32)


def _dot_tn(a, b):
    return lax.dot_general(a, b, (((0,), (0,)), ((), ())), preferred_element_type=F32)


def _split_bf16(x):
    hi = x.astype(BF16)
    lo = (x - hi.astype(F32)).astype(BF16)
    return hi, lo


def _rms(x):
    return x * lax.rsqrt(jnp.mean(x * x, axis=-1, keepdims=True) + EPS)


def _silu(x):
    hx = 0.5 * x
    return hx * jnp.tanh(hx) + hx


def _softplus(x):
    return jnp.maximum(x, 0.0) + jnp.log1p(jnp.exp(-jnp.abs(x)))


def _sqrt_nonneg(x):
    return jnp.where(x > 0.0, x * lax.rsqrt(x), 0.0)


def _col_tiles(w):
    k, n = w.shape
    return w.reshape(k, n // MXU_COLS, MXU_COLS).transpose(1, 0, 2)


def _resident(shape):
    return pl.BlockSpec(shape, lambda *_: (0,) * len(shape), pipeline_mode=pl.Buffered(1))


def _ffn_body(x_ref, g_ref, wup_ref, wdn_ref, gf_ref, o_ref, act_ref, *, final_norm):
    x = x_ref[...]
    xb = (_rms(x) * g_ref[...]).astype(BF16)
    for c in range(FFN_DIM // MXU_COLS):
        lo = c * MXU_COLS
        gate = _dot(xb, wup_ref[:, lo:lo + MXU_COLS])
        up = _dot(xb, wup_ref[:, FFN_DIM + lo:FFN_DIM + lo + MXU_COLS])
        act_ref[:, lo:lo + MXU_COLS] = (_silu(gate) * up).astype(BF16)
    act = act_ref[...]
    y = x + 0.5 * jnp.concatenate([_dot(act, wdn_ref[c]) for c in range(OUT_TILES)], axis=1)
    if final_norm:
        y = _rms(y) * gf_ref[...]
    o_ref[...] = y


def _ffn(x2d, g, w_up, w_down, g_final, *, final_norm, tile_rows):
    rows = x2d.shape[0]
    assert rows % tile_rows == 0
    row_spec = pl.BlockSpec((tile_rows, D_MODEL), lambda i: (i, 0))
    return pl.pallas_call(
        functools.partial(_ffn_body, final_norm=final_norm),
        grid=(rows // tile_rows,),
        in_specs=[row_spec, _resident((1, D_MODEL)), _resident(w_up.shape),
                  _resident(w_down.shape), _resident((1, D_MODEL))],
        out_specs=row_spec,
        out_shape=jax.ShapeDtypeStruct((rows, D_MODEL), F32),
        scratch_shapes=[pltpu.VMEM((tile_rows, FFN_DIM), BF16)],
        compiler_params=pltpu.CompilerParams(dimension_semantics=("arbitrary",),
                                             vmem_limit_bytes=VMEM_LIMIT_BYTES),
        name="ffn_final" if final_norm else "ffn",
    )(x2d, g, w_up, w_down, g_final)


def _stage_tile(buf_ref, tile, first_slab, nb, rows):
    span = TAIL_ROWS + rows
    for half in range(SLABS_PER_TILE):
        for s in range(nb):
            buf_ref[first_slab + half, s * span + TAIL_ROWS:(s + 1) * span, :] = (
                tile[s * rows:(s + 1) * rows, half * LANES:(half + 1) * LANES])


def _conv_slab(buf_ref, slab, w_ref, b_ref, nb, rows):
    span = TAIL_ROWS + rows
    lanes = slice(slab * LANES, (slab + 1) * LANES)
    out = []
    for s in range(nb):
        y = b_ref[:, lanes]
        for k in range(CONV_W):
            start = s * span + TAIL_ROWS - (CONV_W - 1) + k
            y = y + buf_ref[slab, start:start + rows, :] * w_ref[k:k + 1, lanes]
        out.append(y)
    for s in range(nb):
        buf_ref[slab, s * span:s * span + TAIL_ROWS, :] = buf_ref[slab, s * span + rows:(s + 1) * span, :]
    return out[0] if nb == 1 else jnp.concatenate(out, axis=0)


def _linear_scan(a, b, h0):
    rows, width = a.shape
    tiles = rows // SUBLANES
    a3 = a.reshape(tiles, SUBLANES, width)
    b3 = b.reshape(tiles, SUBLANES, width)
    sub = lax.broadcasted_iota(jnp.int32, (tiles, SUBLANES, width), 1)
    shift = 1
    while shift < SUBLANES:
        keep = sub >= shift
        a_prev = pltpu.roll(a3, shift, axis=1)
        b_prev = pltpu.roll(b3, shift, axis=1)
        b3 = jnp.where(keep, a3 * b_prev + b3, b3)
        a3 = jnp.where(keep, a3 * a_prev, a3)
        shift *= 2
    out = []
    carry = h0
    for t in range(tiles):
        h = a3[t] * carry + b3[t]
        out.append(h)
        carry = h[SUBLANES - 1:SUBLANES, :]
    return jnp.concatenate(out, axis=0), carry


def _ssd_seq(xbc_ref, r_base, rows, dt, dt_t, a_row, a_col, state_ref, seq, expand, finish):
    q = min(SSD_CHUNK, rows)
    da = dt * (a_row * LOG2E)
    da_t = dt_t * (a_col * LOG2E)
    log2_dt_t = jnp.log2(dt_t)
    li = lax.broadcasted_iota(jnp.int32, (q, q), 0)
    si = lax.broadcasted_iota(jnp.int32, (q, q), 1)
    causal = li >= si
    lower = causal.astype(BF16)
    upper = (li <= si).astype(BF16)
    lane = lax.broadcasted_iota(jnp.int32, (q, LANES), 1)
    first_head = lane < SSD_HEAD_DIM
    for c in range(rows // q):
        r0 = c * q
        rs = slice(r_base + r0, r_base + r0 + q)
        da_hi, da_lo = _split_bf16(da[r0:r0 + q, :])
        cs = _dot(lower, da_hi) + _dot(lower, da_lo)
        dat_hi, dat_lo = _split_bf16(da_t[:, r0:r0 + q])
        cs_t = _dot(dat_hi, upper) + _dot(dat_lo, upper) - log2_dt_t[:, r0:r0 + q]
        total = cs[q - 1:q, :]
        from_start = jnp.exp2(cs)
        to_end = jnp.exp2(total - cs) * dt[r0:r0 + q, :]
        fs_hi, fs_lo = _split_bf16(from_start)
        from_start_x = _dot(fs_hi, expand) + _dot(fs_lo, expand)
        te_hi, te_lo = _split_bf16(to_end)
        to_end_x = _dot(te_hi, expand) + _dot(te_lo, expand)
        for g in range(SSD_GROUPS):
            c0 = g * GROUP_WIDTH
            b_g = xbc_ref[rs, D_INNER + g * D_STATE:D_INNER + (g + 1) * D_STATE].astype(BF16)
            n_c = D_INNER + (SSD_GROUPS + g) * D_STATE
            c_g = xbc_ref[rs, n_c:n_c + D_STATE].astype(BF16)
            cb = _dot_nt(c_g, b_g)
            y_pairs = []
            for pair in range(GROUP_WIDTH // LANES):
                l0 = c0 + pair * LANES
                x_pair = xbc_ref[rs, l0:l0 + LANES]
                ms, xh = [], []
                for half in range(2):
                    h = l0 // SSD_HEAD_DIM + half
                    seg = cs[:, h:h + 1] - cs_t[h:h + 1, :]
                    ms.append((cb * jnp.exp2(jnp.where(causal, seg, -jnp.inf))).astype(BF16))
                    keep = first_head if half == 0 else jnp.logical_not(first_head)
                    xh.append(jnp.where(keep, x_pair, 0.0).astype(BF16))
                if q % LANES == 0:
                    y_pairs.append(_dot(jnp.concatenate(ms, axis=1), jnp.concatenate(xh, axis=0)))
                else:
                    y_pairs.append(_dot(ms[0], xh[0]) + _dot(ms[1], xh[1]))
            y_diag = jnp.concatenate(y_pairs, axis=1)
            st = state_ref[seq, g]
            fs_g = from_start_x[:, c0:c0 + GROUP_WIDTH]
            y_off = _dot(c_g, st.astype(BF16)) * fs_g
            xw = (xbc_ref[rs, c0:c0 + GROUP_WIDTH] * to_end_x[:, c0:c0 + GROUP_WIDTH]).astype(BF16)
            state_ref[seq, g] = st * fs_g[q - 1:q, :] + _dot_tn(b_g, xw)
            finish(r_base + r0, q, g, y_diag + y_off)


def _mixer_body(*refs, nb, rows, has_state):
    if has_state:
        (x_ref, lru_tail_in, lru_h_in, ssd_tail_in, ssd_h_in, *refs) = refs
    else:
        (x_ref, *refs) = refs
    (g_ref, w_proj, w_dt, w_dt_t, w_gate, b_gate, lconv_w, lconv_b, w_rgig, b_rg, b_ig, lam,
     sconv_w, sconv_b, dt_bias, dt_bias_t, a_log, a_log_t, d_skip, norm_g, w_br_lru, w_br_ssd, w_out,
     o_ref, lru_tail_out, lru_h_out, ssd_tail_out, ssd_h_out,
     lru_h, ssd_state, lru_buf, ssd_buf, xbc_s, zact_s, gate_s, ylru_s, yssd_s, m_s) = refs

    step = pl.program_id(1)
    last = pl.num_programs(1) - 1
    total_rows = nb * rows
    span = TAIL_ROWS + rows

    @pl.when(step == 0)
    def _():
        for s in range(nb):
            if has_state:
                lru_buf[:, s * span:s * span + TAIL_ROWS, :] = lru_tail_in[s]
                ssd_buf[:, s * span:s * span + TAIL_ROWS, :] = ssd_tail_in[s]
                for g in range(SSD_GROUPS):
                    ssd_state[s, g] = ssd_h_in[s, g * GROUP_WIDTH:(g + 1) * GROUP_WIDTH, :].T
            else:
                lru_buf[:, s * span:s * span + TAIL_ROWS, :] = jnp.zeros((LRU_WIDTH // LANES, TAIL_ROWS, LANES), F32)
                ssd_buf[:, s * span:s * span + TAIL_ROWS, :] = jnp.zeros((SSD_CONV_DIM // LANES, TAIL_ROWS, LANES), F32)
        if has_state:
            lru_h[...] = lru_h_in[...]
        else:
            lru_h[...] = jnp.zeros_like(lru_h)
            ssd_state[...] = jnp.zeros_like(ssd_state)

    x = x_ref[...].reshape(total_rows, D_MODEL)
    u = (_rms(x) * g_ref[...]).astype(BF16)

    softplus_neg_lam = _softplus(-lam[...])
    neg_log_a_scale = LRU_C * softplus_neg_lam
    a_exp2_scale = (-LRU_C * LOG2E) * softplus_neg_lam
    xbc_per_lru_tile = XBC_TILES // LRU_TILES
    for c in range(LRU_TILES):
        _stage_tile(lru_buf, _dot(u, w_proj[LRU_X_TILE + c]), c * SLABS_PER_TILE, nb, rows)
        lru_gate = _dot(u, w_proj[LRU_GATE_TILE + c])
        for t in range(c * xbc_per_lru_tile, (c + 1) * xbc_per_lru_tile):
            _stage_tile(ssd_buf, _dot(u, w_proj[XBC_TILE + t]), t * SLABS_PER_TILE, nb, rows)
        for half in range(SLABS_PER_TILE):
            slab = c * SLABS_PER_TILE + half
            cols = slice(half * LANES, (half + 1) * LANES)
            lanes = slice(slab * LANES, (slab + 1) * LANES)
            xc = _conv_slab(lru_buf, slab, lconv_w, lconv_b, nb, rows)
            rg_ig = _dot(xc.astype(BF16), w_rgig[slab])
            r = jax.nn.sigmoid(rg_ig[:, :LRU_BLOCK] + b_rg[:, lanes])
            i_gate = jax.nn.sigmoid(rg_ig[:, LRU_BLOCK:] + b_ig[:, lanes])
            a = jnp.exp2(r * a_exp2_scale[:, lanes])
            one_minus_a2 = jnp.tanh(r * neg_log_a_scale[:, lanes]) * (a * a + 1.0)
            b = _sqrt_nonneg(one_minus_a2) * (i_gate * xc)
            hs = []
            for s in range(nb):
                h_s, h_last = _linear_scan(a[s * rows:(s + 1) * rows], b[s * rows:(s + 1) * rows],
                                           lru_h[s, :, lanes])
                lru_h[s, :, lanes] = h_last
                hs.append(h_s)
            h_all = hs[0] if nb == 1 else jnp.concatenate(hs, axis=0)
            gate = lru_gate[:, cols]
            inner = 0.7978845608028654 * (gate + 0.044715 * (gate * gate * gate))
            hg = (0.5 * gate) * h_all
            ylru_s[:, lanes] = (hg * jnp.tanh(inner) + hg).astype(BF16)

    slabs_per_z = (XBC_TILES * SLABS_PER_TILE) // Z_TILES
    for c in range(Z_TILES):
        cols = slice(c * MXU_COLS, (c + 1) * MXU_COLS)
        zact_s[:, cols] = _silu(_dot(u, w_proj[Z_TILE + c]))
        gate_s[:, cols] = jax.nn.sigmoid(_dot(u, w_gate[c]) + b_gate[:, cols])
        for slab in range(c * slabs_per_z, (c + 1) * slabs_per_z):
            xbc_s[:, slab * LANES:(slab + 1) * LANES] = _silu(_conv_slab(ssd_buf, slab, sconv_w, sconv_b, nb, rows))

    dt = _softplus(_dot(u, w_dt[...]) + dt_bias[...])
    a_row = -jnp.exp(a_log[...])
    a_col = -jnp.exp(a_log_t[...])
    head_of_col = lax.broadcasted_iota(jnp.int32, (SSD_HEADS, D_INNER), 1) // SSD_HEAD_DIM
    expand = (lax.broadcasted_iota(jnp.int32, (SSD_HEADS, D_INNER), 0) == head_of_col).astype(BF16)

    def finish(row0, q, g, y):
        rs = slice(row0, row0 + q)
        cs = slice(g * GROUP_WIDTH, (g + 1) * GROUP_WIDTH)
        y = (y + d_skip[:, cs] * xbc_s[rs, cs]) * zact_s[rs, cs]
        yssd_s[rs, cs] = (_rms(y) * norm_g[:, cs]).astype(BF16)

    for s in range(nb):
        u_s = u[s * rows:(s + 1) * rows]
        dt_t = _softplus(_dot_nt(w_dt_t[...], u_s) + dt_bias_t[...])
        _ssd_seq(xbc_s, s * rows, rows, dt[s * rows:(s + 1) * rows], dt_t, a_row, a_col,
                 ssd_state, s, expand, finish)

    ylru = ylru_s[...]
    yssd = yssd_s[...]
    for c in range(OUT_TILES):
        cols = slice(c * MXU_COLS, (c + 1) * MXU_COLS)
        ssd_cols = slice(D_MODEL + c * MXU_COLS, D_MODEL + (c + 1) * MXU_COLS)
        m = gate_s[:, cols] * _dot(ylru, w_br_lru[c]) + gate_s[:, ssd_cols] * _dot(yssd, w_br_ssd[c])
        m_s[:, cols] = m.astype(BF16)
    m_all = m_s[...]
    for c in range(OUT_TILES):
        cols = slice(c * MXU_COLS, (c + 1) * MXU_COLS)
        out = x[:, cols] + _dot(m_all, w_out[c])
        o_ref[:, :, cols] = out.reshape(nb, rows, MXU_COLS)

    @pl.when(step == last)
    def _():
        lru_h_out[...] = lru_h[...]
        for s in range(nb):
            lru_tail_out[s] = lru_buf[:, s * span:s * span + TAIL_ROWS, :]
            ssd_tail_out[s] = ssd_buf[:, s * span:s * span + TAIL_ROWS, :]
            for g in range(SSD_GROUPS):
                ssd_h_out[s, g * GROUP_WIDTH:(g + 1) * GROUP_WIDTH, :] = ssd_state[s, g].T


def _mixer(x, states, weights, *, nb, rows):
    batch, length, _ = x.shape
    assert batch % nb == 0 and length % rows == 0 and rows % SUBLANES == 0
    has_state = states is not None
    grid = (batch // nb, length // rows)
    lru_slabs = LRU_WIDTH // LANES
    ssd_slabs = SSD_CONV_DIM // LANES
    total_rows = nb * rows

    def per_seq(shape):
        return pl.BlockSpec((nb,) + shape, lambda i, j: (i,) + (0,) * len(shape), pipeline_mode=pl.Buffered(1))

    x_spec = pl.BlockSpec((nb, rows, D_MODEL), lambda i, j: (i, j, 0))
    state_shapes = [(lru_slabs, TAIL_ROWS, LANES), (1, LRU_WIDTH), (ssd_slabs, TAIL_ROWS, LANES), (D_INNER, D_STATE)]
    state_specs = [per_seq(s) for s in state_shapes]
    in_specs = [x_spec] + (state_specs if has_state else []) + [_resident(w.shape) for w in weights]
    out_shape = [jax.ShapeDtypeStruct((batch, length, D_MODEL), F32)] + [
        jax.ShapeDtypeStruct((batch,) + s, F32) for s in state_shapes]
    scratch = [pltpu.VMEM((nb, 1, LRU_WIDTH), F32),
               pltpu.VMEM((nb, SSD_GROUPS, D_STATE, GROUP_WIDTH), F32),
               pltpu.VMEM((lru_slabs, nb * (TAIL_ROWS + rows), LANES), F32),
               pltpu.VMEM((ssd_slabs, nb * (TAIL_ROWS + rows), LANES), F32),
               pltpu.VMEM((total_rows, SSD_CONV_DIM), F32),
               pltpu.VMEM((total_rows, D_INNER), F32),
               pltpu.VMEM((total_rows, 2 * D_MODEL), F32),
               pltpu.VMEM((total_rows, LRU_WIDTH), BF16),
               pltpu.VMEM((total_rows, D_INNER), BF16),
               pltpu.VMEM((total_rows, D_MODEL), BF16)]
    args = (x,) + (tuple(states) if has_state else ()) + tuple(weights)
    return pl.pallas_call(
        functools.partial(_mixer_body, nb=nb, rows=rows, has_state=has_state),
        grid=grid,
        in_specs=in_specs,
        out_specs=[x_spec] + state_specs,
        out_shape=out_shape,
        scratch_shapes=scratch,
        compiler_params=pltpu.CompilerParams(dimension_semantics=("arbitrary", "arbitrary"),
                                             vmem_limit_bytes=VMEM_LIMIT_BYTES),
        name="mixer_state" if has_state else "mixer",
    )(*args)


def _tail_to_slabs(conv_state):
    b, _, c = conv_state.shape
    padded = jnp.pad(conv_state, ((0, 0), (TAIL_ROWS - (CONV_W - 1), 0), (0, 0)))
    return padded.reshape(b, TAIL_ROWS, c // LANES, LANES).transpose(0, 2, 1, 3)


def _slabs_to_tail(slabs):
    b, n, _, _ = slabs.shape
    rows = slabs[:, :, TAIL_ROWS - (CONV_W - 1):, :].transpose(0, 2, 1, 3)
    return rows.reshape(1, b, CONV_W - 1, n * LANES)


def kernel(x_prompt, x_sample, state_lru_conv, state_lru_h, state_ssd_conv, state_ssd_h, norm_ffn1_g, ffn1_w_up, ffn1_w_down, norm_mix_g, w_in, b_gate, lru_conv_w, lru_conv_b, lru_w_rg, lru_b_rg, lru_w_ig, lru_b_ig, lru_lambda, ssd_conv_w, ssd_conv_b, ssd_dt_bias, ssd_A_log, ssd_D, ssd_norm_g, w_br_lru, w_br_ssd, w_out, norm_ffn2_g, ffn2_w_up, ffn2_w_down, norm_final_g):
    depth = w_in.shape[0]
    assert depth == 1
    l = 0
    row = lambda v: v.reshape(1, -1)
    off_dt = 2 * LRU_WIDTH + D_INNER + SSD_CONV_DIM
    w_in_l = w_in[l].astype(BF16)
    w_dt = w_in_l[:, off_dt:off_dt + SSD_HEADS]
    mixer_weights = (
        row(norm_mix_g[l]),
        _col_tiles(w_in_l[:, :off_dt]),
        w_dt,
        w_dt.T,
        _col_tiles(w_in_l[:, off_dt + SSD_HEADS:]),
        row(b_gate[l]),
        lru_conv_w[l], row(lru_conv_b[l]),
        jnp.concatenate([lru_w_rg[l], lru_w_ig[l]], axis=-1).astype(BF16),
        row(lru_b_rg[l]), row(lru_b_ig[l]), row(lru_lambda[l]),
        ssd_conv_w[l], row(ssd_conv_b[l]),
        row(ssd_dt_bias[l]), ssd_dt_bias[l].reshape(-1, 1),
        row(ssd_A_log[l]), ssd_A_log[l].reshape(-1, 1),
        row(jnp.repeat(ssd_D[l], SSD_HEAD_DIM)),
        row(ssd_norm_g[l]),
        _col_tiles(w_br_lru[l].astype(BF16)), _col_tiles(w_br_ssd[l].astype(BF16)),
        _col_tiles(w_out[l].astype(BF16)),
    )
    ffn1 = (row(norm_ffn1_g[l]), ffn1_w_up[l].astype(BF16), _col_tiles(ffn1_w_down[l].astype(BF16)))
    ffn2 = (row(norm_ffn2_g[l]), ffn2_w_up[l].astype(BF16), _col_tiles(ffn2_w_down[l].astype(BF16)))
    g_final = row(norm_final_g)

    def layer(x, states, *, ffn_rows, nb, rows):
        batch, length, _ = x.shape
        h = _ffn(x.reshape(batch * length, D_MODEL), *ffn1, g_final, final_norm=False, tile_rows=ffn_rows)
        h, lru_tail, lru_h, ssd_tail, ssd_h = _mixer(h.reshape(batch, length, D_MODEL), states, mixer_weights,
                                                     nb=nb, rows=rows)
        y = _ffn(h.reshape(batch * length, D_MODEL), *ffn2, g_final, final_norm=True, tile_rows=ffn_rows)
        new_states = (_slabs_to_tail(lru_tail),
                      lru_h.reshape(1, batch, LRU_WIDTH),
                      _slabs_to_tail(ssd_tail),
                      ssd_h.reshape(1, batch, SSD_HEADS, SSD_HEAD_DIM, D_STATE))
        return y.reshape(batch, length, D_MODEL), new_states

    y_prompt, st_p = layer(x_prompt, None, ffn_rows=512, nb=1, rows=256)
    dec_batch, dec_len, _ = x_sample.shape
    sample_states = (_tail_to_slabs(state_lru_conv[l]), state_lru_h[l].reshape(dec_batch, 1, LRU_WIDTH),
                     _tail_to_slabs(state_ssd_conv[l]), state_ssd_h[l].reshape(dec_batch, D_INNER, D_STATE))
    y_sample, st_s = layer(x_sample, sample_states, ffn_rows=dec_batch * dec_len, nb=dec_batch // 2, rows=dec_len)
    return (y_prompt, y_sample) + st_p + st_s
```
